```python
import math
import jax, jax.numpy as jnp
from jax import lax
import numpy as np

D_MODEL = 1024
BATCH = 8
SEQ = 4096
DEPTH = 1

N_MEM = 256
MEM_HEADS = 4
MEM_HEAD_DIM = 128
MEM_WIDTH = MEM_HEADS * MEM_HEAD_DIM
D_RNN = 3 * D_MODEL // 4
LRU_BLOCK = 64
N_LRU_BLOCKS = D_RNN // LRU_BLOCK
CONV_WIDTH = 4
LRU_C = 8.0
DIL_GROUPS = ((128, 1), (512, 4), (2048, 16))
N_DIL_GROUPS = len(DIL_GROUPS)
HEADS_PER_GROUP = 4
DIL_HEAD_DIM = 64
N_DIL_HEADS = N_DIL_GROUPS * HEADS_PER_GROUP
DIL_QKV_WIDTH = 3 * N_DIL_HEADS * DIL_HEAD_DIM
DIL_OUT_WIDTH = HEADS_PER_GROUP * DIL_HEAD_DIM
NUM_BUCKETS = 32
MAX_DISTANCE = 2048
N_BRANCHES = 3
D_FF = 4 * D_MODEL
EPS = 1e-6
NEG = -1e30

SPLITS = (D_RNN, 2 * D_RNN, 2 * D_RNN + DIL_QKV_WIDTH, 2 * D_RNN + DIL_QKV_WIDTH + MEM_WIDTH)
D_IN = 2 * D_RNN + DIL_QKV_WIDTH + MEM_WIDTH + N_BRANCHES * D_MODEL

kernel_name = "hybrid_rglru_dilated_attn_memxattn_block"


def _rmsnorm(x, g):
    x32 = x.astype(jnp.float32)
    y = x32 * lax.rsqrt(jnp.mean(x32 * x32, axis=-1, keepdims=True) + EPS)
    return (y * g.astype(jnp.float32)).astype(x.dtype)


def _t5_bucket(dist):
    max_exact = NUM_BUCKETS // 2
    df = jnp.maximum(dist, 1).astype(jnp.float32)
    large = max_exact + (jnp.log(df / max_exact) / math.log(MAX_DISTANCE / max_exact)
                         * (NUM_BUCKETS - max_exact)).astype(jnp.int32)
    large = jnp.minimum(large, NUM_BUCKETS - 1)
    return jnp.where(dist < max_exact, dist, large)


def _rg_lru(xc, w_a, b_a, w_x, b_x, lam):
    B, S, C = xc.shape
    xb = xc.reshape(B, S, N_LRU_BLOCKS, LRU_BLOCK)
    r = jax.nn.sigmoid((jnp.einsum('bshi,hij->bshj', xb, w_a).reshape(B, S, C) + b_a).astype(jnp.float32))
    i = jax.nn.sigmoid((jnp.einsum('bshi,hij->bshj', xb, w_x).reshape(B, S, C) + b_x).astype(jnp.float32))
    log_a = -LRU_C * r * jax.nn.softplus(-lam.astype(jnp.float32))
    a = jnp.exp(log_a)
    mult = jnp.sqrt(-jnp.expm1(2.0 * log_a))
    b = mult * (i * xc.astype(jnp.float32))

    def combine(left, right):
        a1, b1 = left
        a2, b2 = right
        return a1 * a2, a2 * b1 + b2

    _, h = lax.associative_scan(combine, (a, b), axis=1)
    return h


def _dilated_group(q, k, v, table_g, dilation, span):
    B, S, H, Dh = q.shape
    L = S // dilation
    nb = -(-L // span)
    Lp = nb * span

    def to_blocks(t):
        t = t.reshape(B, L, dilation, H, Dh).transpose(0, 2, 3, 1, 4)
        t = jnp.pad(t, ((0, 0), (0, 0), (0, 0), (0, Lp - L), (0, 0)))
        return t.reshape(B, dilation, H, nb, span, Dh)

    def with_prev(t):
        prev = jnp.pad(t, ((0, 0), (0, 0), (0, 0), (1, 0), (0, 0), (0, 0)))[:, :, :, :-1]
        return jnp.concatenate([prev, t], axis=-2)

    qb = to_blocks(q)
    kk = with_prev(to_blocks(k))
    vv = with_prev(to_blocks(v))

    qi = jnp.arange(span)[:, None]
    kj = jnp.arange(2 * span)[None, :]
    off = qi + span - kj
    valid = (off >= 0) & (off <= span)
    mask = valid[None] & ((jnp.arange(nb)[:, None, None] > 0) | (kj >= span)[None])
    bucket = _t5_bucket(jnp.maximum(off, 0) * dilation)
    bias = jnp.transpose(table_g.astype(jnp.float32)[bucket], (2, 0, 1))

    s = jnp.einsum('brhnqc,brhnkc->brhnqk', qb, kk) * (DIL_HEAD_DIM ** -0.5) + bias[None, None, :, None]
    s = jnp.where(mask[None, None, None], s, NEG)
    m = jnp.max(s, axis=-1, keepdims=True)
    p = jnp.exp(s - m)
    den = jnp.sum(p, axis=-1)
    o = jnp.einsum('brhnqk,brhnkc->brhnqc', p, vv) / den[..., None]
    lse = m[..., 0] + jnp.log(den)

    o = o.reshape(B, dilation, H, Lp, Dh)[:, :, :, :L].transpose(0, 3, 1, 2, 4).reshape(B, S, H, Dh)
    lse = lse.reshape(B, dilation, H, Lp)[:, :, :, :L].transpose(0, 3, 1, 2).reshape(B, S, H)
    return o, lse


def setup_inputs(seed: int = 0) -> dict:
    key = jax.random.key(seed)
    ks = jax.random.split(key, 24)
    f32 = jnp.float32

    def nrm(k, shape, fan_in):
        return jax.random.normal(k, shape, f32) * (fan_in ** -0.5)

    def gain(k, n):
        return 1.0 + 0.05 * jax.random.normal(k, (n,), f32)

    u = jax.random.uniform(ks[11], (D_RNN,), f32, 0.9, 0.999)
    sig = u ** (1.0 / LRU_C)
    lam = jnp.log(sig) - jnp.log1p(-sig)
    return {
        "x": jax.random.normal(ks[0], (BATCH, SEQ, D_MODEL), f32),
        "mem": jax.random.normal(ks[1], (BATCH, N_MEM, D_MODEL), f32),
        "g_mix": gain(ks[2], D_MODEL),
        "w_in": nrm(ks[3], (D_MODEL, D_IN), D_MODEL),
        "b_gate": 0.01 * jax.random.normal(ks[4], (N_BRANCHES * D_MODEL,), f32),
        "conv_w": nrm(ks[5], (CONV_WIDTH, D_RNN), CONV_WIDTH),
        "conv_b": 0.01 * jax.random.normal(ks[6], (D_RNN,), f32),
        "w_rg_a": nrm(ks[7], (N_LRU_BLOCKS, LRU_BLOCK, LRU_BLOCK), LRU_BLOCK),
        "b_rg_a": 0.01 * jax.random.normal(ks[8], (D_RNN,), f32),
        "w_rg_x": nrm(ks[9], (N_LRU_BLOCKS, LRU_BLOCK, LRU_BLOCK), LRU_BLOCK),
        "b_rg_x": 0.01 * jax.random.normal(ks[10], (D_RNN,), f32),
        "lru_lambda": lam,
        "w_lru_out": nrm(ks[12], (D_RNN, D_MODEL), D_RNN),
        "rel_bias": 0.1 * jax.random.normal(ks[13], (NUM_BUCKETS, N_DIL_HEADS), f32),
        "w_dil_out": nrm(ks[14], (DIL_OUT_WIDTH, D_MODEL), DIL_OUT_WIDTH),
        "g_mem": gain(ks[15], D_MODEL),
        "w_mem_kv": nrm(ks[16], (D_MODEL, 2 * MEM_WIDTH), D_MODEL),
        "w_mem_out": nrm(ks[17], (MEM_WIDTH, D_MODEL), MEM_WIDTH),
        "w_out": nrm(ks[18], (D_MODEL, D_MODEL), D_MODEL),
        "g_mlp": gain(ks[19], D_MODEL),
        "w_mlp_in": nrm(ks[20], (D_MODEL, D_FF), D_MODEL),
        "w_mlp_out": nrm(ks[21], (D_FF, D_MODEL), D_FF),
        "g_final": gain(ks[22], D_MODEL),
    }


def reference(x, mem, g_mix, w_in, b_gate, conv_w, conv_b, w_rg_a, b_rg_a, w_rg_x, b_rg_x,
              lru_lambda, w_lru_out, rel_bias, w_dil_out, g_mem, w_mem_kv, w_mem_out, w_out,
              g_mlp, w_mlp_in, w_mlp_out, g_final):
    B, S, D = x.shape
    dt = x.dtype
    f32 = jnp.float32
    mem_n = _rmsnorm(mem, g_mem)

    for _ in range(DEPTH):
        h = _rmsnorm(x, g_mix)
        proj = h @ w_in
        x_lru, gate_lru, qkv, q_mem, gates = jnp.split(proj, SPLITS, axis=-1)

        xc = lax.conv_general_dilated(x_lru, conv_w[:, None, :].astype(x_lru.dtype), window_strides=(1,),
                                      padding=[(CONV_WIDTH - 1, 0)], dimension_numbers=('NWC', 'WIO', 'NWC'),
                                      feature_group_count=D_RNN) + conv_b
        hl = _rg_lru(xc, w_rg_a, b_rg_a, w_rg_x, b_rg_x, lru_lambda)
        y_lru = (jax.nn.gelu(gate_lru.astype(f32)) * hl).astype(dt) @ w_lru_out

        qkv = qkv.astype(f32).reshape(B, S, 3, N_DIL_GROUPS, HEADS_PER_GROUP, DIL_HEAD_DIM)
        outs, lses = [], []
        for g, (window, dil) in enumerate(DIL_GROUPS):
            o_g, lse_g = _dilated_group(qkv[:, :, 0, g], qkv[:, :, 1, g], qkv[:, :, 2, g],
                                        rel_bias[:, g * HEADS_PER_GROUP:(g + 1) * HEADS_PER_GROUP],
                                        dil, window // dil)
            outs.append(o_g)
            lses.append(lse_g)
        alpha = jax.nn.softmax(jnp.stack(lses, axis=0), axis=0)
        o_dil = jnp.sum(alpha[..., None] * jnp.stack(outs, axis=0), axis=0)
        y_dil = o_dil.reshape(B, S, DIL_OUT_WIDTH).astype(dt) @ w_dil_out

        kv = (mem_n @ w_mem_kv).astype(f32).reshape(B, N_MEM, 2, MEM_HEADS, MEM_HEAD_DIM)
        qm = q_mem.astype(f32).reshape(B, S, MEM_HEADS, MEM_HEAD_DIM)
        sm = jnp.einsum('bqhc,bkhc->bhqk', qm, kv[:, :, 0]) * (MEM_HEAD_DIM ** -0.5)
        pm = jax.nn.softmax(sm, axis=-1)
        om = jnp.einsum('bhqk,bkhc->bqhc', pm, kv[:, :, 1]).reshape(B, S, MEM_WIDTH)
        y_mem = om.astype(dt) @ w_mem_out

        gt = jax.nn.sigmoid((gates + b_gate).astype(f32)).reshape(B, S, N_BRANCHES, D)
        merged = (gt[:, :, 0] * y_lru.astype(f32) + gt[:, :, 1] * y_dil.astype(f32)
                  + gt[:, :, 2] * y_mem.astype(f32)).astype(dt)
        x = x + merged @ w_out

        hm = _rmsnorm(x, g_mlp)
        x = x + jnp.square(jax.nn.relu(hm @ w_mlp_in)) @ w_mlp_out

    return _rmsnorm(x, g_final)
```

```python
import functools
import math

import numpy as np
import jax
import jax.numpy as jnp
from jax import lax
from jax.experimental import pallas as pl
from jax.experimental.pallas import tpu as pltpu

D_MODEL = 1024
N_MEM = 256
MEM_HEADS = 4
MEM_HEAD_DIM = 128
MEM_WIDTH = MEM_HEADS * MEM_HEAD_DIM
D_RNN = 768
LRU_BLOCK = 64
N_LRU_BLOCKS = D_RNN // LRU_BLOCK
CONV_WIDTH = 4
LRU_C = 8.0
DIL_GROUPS = ((128, 1), (512, 4), (2048, 16))
N_DIL_GROUPS = 3
HEADS_PER_GROUP = 4
DIL_HEAD_DIM = 64
N_DIL_HEADS = 12
GROUP_WIDTH = HEADS_PER_GROUP * DIL_HEAD_DIM
DIL_QKV_WIDTH = 3 * N_DIL_HEADS * DIL_HEAD_DIM
SPAN = 128
NUM_BUCKETS = 32
MAX_DISTANCE = 2048
N_BRANCHES = 3
D_FF = 4 * D_MODEL
EPS = 1e-6
NEG = -1e30

LANES = 128
SUBLANES = 8
ATTN_TILE = SPAN * 16
VMEM_LIMIT = 56 * 1024 * 1024

BF16 = jnp.bfloat16
F32 = jnp.float32


def _rms(x32, g):
    return x32 * lax.rsqrt(jnp.mean(x32 * x32, axis=-1, keepdims=True) + EPS) * g


def _const_spec(shape):
    nd = len(shape)
    return pl.BlockSpec(shape, lambda *_: (0,) * nd, pipeline_mode=pl.Buffered(1))


def _params(*sem):
    return pltpu.CompilerParams(dimension_semantics=sem, vmem_limit_bytes=VMEM_LIMIT)


def _mem_kv_kernel(mem_ref, g_ref, w_ref, k_ref, v_ref):
    mn = _rms(mem_ref[0], g_ref[...]).astype(BF16)
    kv = jnp.dot(mn, w_ref[...], preferred_element_type=F32)
    k_ref[0] = kv[:, :MEM_WIDTH].astype(BF16)
    v_ref[0] = kv[:, MEM_WIDTH:].astype(BF16)


def _mem_kv(mem, g_mem, w_kv):
    B = mem.shape[0]
    return pl.pallas_call(
        _mem_kv_kernel,
        grid=(B,),
        in_specs=[pl.BlockSpec((1, N_MEM, D_MODEL), lambda b: (b, 0, 0)),
                  _const_spec((1, D_MODEL)),
                  _const_spec((D_MODEL, 2 * MEM_WIDTH))],
        out_specs=[pl.BlockSpec((1, N_MEM, MEM_WIDTH), lambda b: (b, 0, 0))] * 2,
        out_shape=[jax.ShapeDtypeStruct((B, N_MEM, MEM_WIDTH), BF16)] * 2,
        compiler_params=_params("arbitrary"),
        name="mem_kv",
    )(mem, g_mem, w_kv)


IN_PROJ_WIDTHS = (D_RNN, D_RNN) + (GROUP_WIDTH, 2 * GROUP_WIDTH) * N_DIL_GROUPS + (MEM_WIDTH,)


def _in_proj_kernel(x_ref, g_ref, w_ref, *out_refs):
    h = _rms(x_ref[...], g_ref[...]).astype(BF16)
    col = 0
    for ref in out_refs:
        n = ref.shape[-1]
        ref[...] = jnp.dot(h, w_ref[:, col:col + n], preferred_element_type=F32).astype(ref.dtype)
        col += n


def _in_proj(x2, g_mix, w_a, tm=512):
    N = x2.shape[0]
    return pl.pallas_call(
        _in_proj_kernel,
        grid=(N // tm,),
        in_specs=[pl.BlockSpec((tm, D_MODEL), lambda i: (i, 0)),
                  _const_spec((1, D_MODEL)),
                  _const_spec(w_a.shape)],
        out_specs=[pl.BlockSpec((tm, n), lambda i: (i, 0)) for n in IN_PROJ_WIDTHS],
        out_shape=[jax.ShapeDtypeStruct((N, n), BF16) for n in IN_PROJ_WIDTHS],
        compiler_params=_params("arbitrary"),
        name="in_proj",
    )(x2, g_mix, w_a)


def _lru_kernel(xl_ref, gl_ref, cw_ref, cb_ref, wab_ref, ba_ref, bx_ref, lam_ref, u_ref,
                xpad, a_s, b_s, hc):
    tt = xl_ref.shape[0]
    t = pl.program_id(1)

    @pl.when(t == 0)
    def _():
        xpad[0:SUBLANES, :] = jnp.zeros((SUBLANES, D_RNN), F32)
        hc[...] = jnp.zeros_like(hc)

    xl = xl_ref[...].astype(F32)
    xpad[SUBLANES:SUBLANES + tt, :] = xl
    xc = cb_ref[...] + cw_ref[CONV_WIDTH - 1:CONV_WIDTH, :] * xl
    for j in range(CONV_WIDTH - 1):
        lo = SUBLANES - (CONV_WIDTH - 1) + j
        xc = xc + cw_ref[j:j + 1, :] * xpad[lo:lo + tt, :]
    xpad[0:SUBLANES, :] = xl[tt - SUBLANES:tt, :]

    sp = jax.nn.softplus(-lam_ref[...])
    xcb = xc.astype(BF16)
    for c in range(D_RNN // LANES):
        sl = slice(c * LANES, (c + 1) * LANES)
        z = jnp.dot(xcb[:, sl], wab_ref[c], preferred_element_type=F32)
        r = jax.nn.sigmoid(z[:, :LANES] + ba_ref[:, sl])
        i = jax.nn.sigmoid(z[:, LANES:] + bx_ref[:, sl])
        log_a = (-LRU_C) * r * sp[:, sl]
        a = jnp.exp(log_a)
        a_s[:, sl] = a
        b_s[:, sl] = jnp.sqrt(1.0 - a * a) * (i * xc[:, sl])

    row = lax.broadcasted_iota(jnp.int32, (SUBLANES, D_RNN), 0)

    def group(i, hprev):
        r0 = pl.multiple_of(i * SUBLANES, SUBLANES)
        a = a_s[pl.ds(r0, SUBLANES), :]
        b = b_s[pl.ds(r0, SUBLANES), :]
        for s in (1, 2, 4):
            a_sh = jnp.where(row >= s, pltpu.roll(a, s, 0), 1.0)
            b_sh = jnp.where(row >= s, pltpu.roll(b, s, 0), 0.0)
            b = a * b_sh + b
            a = a * a_sh
        h = b + a * hprev
        b_s[pl.ds(r0, SUBLANES), :] = h
        return jnp.broadcast_to(h[SUBLANES - 1:SUBLANES, :], (SUBLANES, D_RNN))

    hc[...] = lax.fori_loop(0, tt // SUBLANES, group, hc[...])

    g = gl_ref[...].astype(F32)
    gelu = 0.5 * g * (1.0 + jnp.tanh(math.sqrt(2.0 / math.pi) * (g + 0.044715 * (g * g * g))))
    u_ref[...] = (gelu * b_s[...]).astype(BF16)


def _lru(xl, gl, conv_w, conv_b, wab, b_a, b_x, lam, B, S, tt=512):
    nt = S // tt
    row_spec = pl.BlockSpec((tt, D_RNN), lambda b, t: (b * nt + t, 0))
    return pl.pallas_call(
        _lru_kernel,
        grid=(B, nt),
        in_specs=[row_spec, row_spec,
                  _const_spec((CONV_WIDTH, D_RNN)), _const_spec((1, D_RNN)),
                  _const_spec(wab.shape),
                  _const_spec((1, D_RNN)), _const_spec((1, D_RNN)), _const_spec((1, D_RNN))],
        out_specs=row_spec,
        out_shape=jax.ShapeDtypeStruct((B * S, D_RNN), BF16),
        scratch_shapes=[pltpu.VMEM((tt + SUBLANES, D_RNN), F32),
                        pltpu.VMEM((tt, D_RNN), F32),
                        pltpu.VMEM((tt, D_RNN), F32),
                        pltpu.VMEM((SUBLANES, D_RNN), F32)],
        compiler_params=_params("arbitrary", "arbitrary"),
        name="lru",
    )(xl, gl, conv_w, conv_b, wab, b_a, b_x, lam)


def _bucket_tables():
    kj = np.arange(2 * SPAN)[:, None]
    qi = np.arange(SPAN)[None, :]
    off = qi + SPAN - kj
    valid = (off >= 0) & (off <= SPAN)
    max_exact = NUM_BUCKETS // 2
    tabs = []
    for _, dil in DIL_GROUPS:
        dist = np.maximum(off, 0) * dil
        df = np.maximum(dist, 1).astype(np.float32)
        large = max_exact + (np.log(df / max_exact) / math.log(MAX_DISTANCE / max_exact)
                             * (NUM_BUCKETS - max_exact)).astype(np.int32)
        large = np.minimum(large, NUM_BUCKETS - 1)
        bucket = np.where(dist < max_exact, dist, large)
        tabs.append(np.where(valid, bucket, -1).astype(np.int32))
    return np.stack(tabs)


def _rel_bias_kernel(tab_ref, bucket_ref, out_ref):
    hd = pl.program_id(0)
    bk = bucket_ref[0]
    acc = jnp.full(bk.shape, NEG, F32)
    for b in range(NUM_BUCKETS):
        acc = jnp.where(bk == b, tab_ref[b * N_DIL_HEADS + hd], acc)
    out_ref[0] = acc


def _rel_bias(rel_bias):
    buckets = jnp.asarray(_bucket_tables())
    return pl.pallas_call(
        _rel_bias_kernel,
        grid=(N_DIL_HEADS,),
        in_specs=[pl.BlockSpec(memory_space=pltpu.SMEM),
                  pl.BlockSpec((1, 2 * SPAN, SPAN), lambda h: (h // HEADS_PER_GROUP, 0, 0))],
        out_specs=pl.BlockSpec((1, 2 * SPAN, SPAN), lambda h: (h, 0, 0)),
        out_shape=jax.ShapeDtypeStruct((N_DIL_HEADS, 2 * SPAN, SPAN), F32),
        compiler_params=_params("arbitrary"),
        name="rel_bias",
    )(rel_bias.reshape(-1), buckets)


def _dil_attn_kernel(q_ref, kv_ref, kvp_ref, bias_ref, o_ref, lse_ref, *, n_res, n_blk):
    first_tile = pl.program_id(1) == 0
    lane = lax.broadcasted_iota(jnp.int32, (SPAN, LANES), 1)
    key_row = lax.broadcasted_iota(jnp.int32, (2 * SPAN, SPAN), 0)
    pair_dims = (((1,), (1,)), ((), ()))
    tr_lhs_dims = (((0,), (0,)), ((), ()))

    def block(q2, kcat, vcat, pair, mask_prev):
        qq = jnp.concatenate([jnp.where(lane < DIL_HEAD_DIM, q2, 0),
                              jnp.where(lane >= DIL_HEAD_DIM, q2, 0)], axis=0)
        s_all = lax.dot_general(kcat, qq, pair_dims, preferred_element_type=F32)
        o_t, lse_t = [], []
        for e in range(2):
            s = s_all[:, e * SPAN:(e + 1) * SPAN] + bias_ref[2 * pair + e]
            if mask_prev:
                s = jnp.where(jnp.logical_and(first_tile, key_row < SPAN), NEG, s)
            m = jnp.max(s, axis=0, keepdims=True)
            p = jnp.exp(s - m)
            den = jnp.sum(p, axis=0, keepdims=True)
            ov = lax.dot_general(vcat, p.astype(BF16), tr_lhs_dims, preferred_element_type=F32)
            o_t.append(ov[e * DIL_HEAD_DIM:(e + 1) * DIL_HEAD_DIM, :] / den)
            lse_t.append(jnp.broadcast_to(m + jnp.log(den), (DIL_HEAD_DIM, SPAN)))
        return jnp.concatenate(o_t, axis=0).T, jnp.concatenate(lse_t, axis=0).T

    for r in range(n_res):
        for pair in range(2):
            qc = slice(r * GROUP_WIDTH + pair * LANES, r * GROUP_WIDTH + (pair + 1) * LANES)
            kc = slice(r * 2 * GROUP_WIDTH + pair * LANES, r * 2 * GROUP_WIDTH + (pair + 1) * LANES)
            vc = slice(kc.start + GROUP_WIDTH, kc.stop + GROUP_WIDTH)

            kcat = jnp.concatenate([kvp_ref[0, :, kc], kv_ref[0, 0:SPAN, kc]], axis=0)
            vcat = jnp.concatenate([kvp_ref[0, :, vc], kv_ref[0, 0:SPAN, vc]], axis=0)
            o, lse = block(q_ref[0, 0:SPAN, qc], kcat, vcat, pair, True)
            o_ref[0, 0:SPAN, qc] = o
            lse_ref[0, 0:SPAN, qc] = lse

            if n_blk > 1:
                def body(n, carry, qc=qc, kc=kc, vc=vc, pair=pair):
                    q0 = pl.multiple_of(n * SPAN, SPAN)
                    k0 = pl.multiple_of((n - 1) * SPAN, SPAN)
                    o, lse = block(q_ref[0, pl.ds(q0, SPAN), qc],
                                   kv_ref[0, pl.ds(k0, 2 * SPAN), kc],
                                   kv_ref[0, pl.ds(k0, 2 * SPAN), vc], pair, False)
                    o_ref[0, pl.ds(q0, SPAN), qc] = o
                    lse_ref[0, pl.ds(q0, SPAN), qc] = lse
                    return carry
                lax.fori_loop(1, n_blk, body, 0)


def _dil_attn(q, kv, bias_g, B, S, dil):
    L = S // dil
    rows = ATTN_TILE // dil
    n_blk = rows // SPAN
    n_res = min(dil, 4)
    qv = q.reshape(B, L, dil * GROUP_WIDTH)
    kvv = kv.reshape(B, L, dil * 2 * GROUP_WIDTH)
    q_spec = pl.BlockSpec((1, rows, n_res * GROUP_WIDTH), lambda b, t, c: (b, t, c))
    kv_spec = pl.BlockSpec((1, rows, n_res * 2 * GROUP_WIDTH), lambda b, t, c: (b, t, c))
    kvp_spec = pl.BlockSpec((1, SPAN, n_res * 2 * GROUP_WIDTH),
                            lambda b, t, c: (b, jnp.maximum(t * n_blk - 1, 0), c))
    o, lse = pl.pallas_call(
        functools.partial(_dil_attn_kernel, n_res=n_res, n_blk=n_blk),
        grid=(B, S // ATTN_TILE, dil // n_res),
        in_specs=[q_spec, kv_spec, kvp_spec, _const_spec(bias_g.shape)],
        out_specs=[q_spec, q_spec],
        out_shape=[jax.ShapeDtypeStruct(qv.shape, F32)] * 2,
        compiler_params=_params("arbitrary", "arbitrary", "arbitrary"),
        name=f"dil_attn_d{dil}",
    )(qv, kvv, kvv, bias_g)
    return o.reshape(B * S, GROUP_WIDTH), lse.reshape(B * S, GROUP_WIDTH)


def _merge_kernel(x_ref, u_ref, o1_ref, l1_ref, o4_ref, l4_ref, o16_ref, l16_ref, qm_ref, mk_ref, mv_ref,
                  g_ref, wg_ref, bg_ref, wl_ref, wd_ref, wm_ref, wo_ref, out_ref):
    x = x_ref[...]
    h = _rms(x, g_ref[...]).astype(BF16)

    def gate(j):
        z = jnp.dot(h, wg_ref[:, j * D_MODEL:(j + 1) * D_MODEL], preferred_element_type=F32)
        return jax.nn.sigmoid(z + bg_ref[:, j * D_MODEL:(j + 1) * D_MODEL])

    merged = gate(0) * jnp.dot(u_ref[...], wl_ref[...], preferred_element_type=F32)

    l1, l4, l16 = l1_ref[...], l4_ref[...], l16_ref[...]
    mx = jnp.maximum(jnp.maximum(l1, l4), l16)
    e1, e4, e16 = jnp.exp(l1 - mx), jnp.exp(l4 - mx), jnp.exp(l16 - mx)
    o_dil = (e1 * o1_ref[...] + e4 * o4_ref[...] + e16 * o16_ref[...]) / (e1 + e4 + e16)
    merged = merged + gate(1) * jnp.dot(o_dil.astype(BF16), wd_ref[...], preferred_element_type=F32)

    om = []
    for hd in range(MEM_HEADS):
        sl = slice(hd * MEM_HEAD_DIM, (hd + 1) * MEM_HEAD_DIM)
        s = lax.dot_general(qm_ref[:, sl], mk_ref[0, :, sl], (((1,), (1,)), ((), ())),
                            preferred_element_type=F32) * (MEM_HEAD_DIM ** -0.5)
        p = jnp.exp(s - jnp.max(s, axis=-1, keepdims=True))
        den = jnp.sum(p, axis=-1, keepdims=True)
        om.append(jnp.dot(p.astype(BF16), mv_ref[0, :, sl], preferred_element_type=F32) / den)
    om = jnp.concatenate(om, axis=-1).astype(BF16)
    merged = merged + gate(2) * jnp.dot(om, wm_ref[...], preferred_element_type=F32)

    out_ref[...] = x + jnp.dot(merged.astype(BF16), wo_ref[...], preferred_element_type=F32)


def _merge(x2, u, dil_outs, qm, mk, mv, g_mix, w_g, b_gate, w_l, w_d, w_m, w_o, S, tm=512):
    N = x2.shape[0]
    nt = S // tm
    def rows(n):
        return pl.BlockSpec((tm, n), lambda i: (i, 0))
    mem_spec = pl.BlockSpec((1, N_MEM, MEM_WIDTH), lambda i: (i // nt, 0, 0))
    return pl.pallas_call(
        _merge_kernel,
        grid=(N // tm,),
        in_specs=[rows(D_MODEL), rows(D_RNN)] + [rows(GROUP_WIDTH)] * 6 + [rows(MEM_WIDTH), mem_spec, mem_spec,
                  _const_spec((1, D_MODEL)), _const_spec(w_g.shape), _const_spec((1, N_BRANCHES * D_MODEL)),
                  _const_spec(w_l.shape), _const_spec(w_d.shape), _const_spec(w_m.shape), _const_spec(w_o.shape)],
        out_specs=rows(D_MODEL),
        out_shape=jax.ShapeDtypeStruct((N, D_MODEL), F32),
        compiler_params=_params("arbitrary"),
        name="merge",
    )(x2, u, *dil_outs, qm, mk, mv, g_mix, w_g, b_gate, w_l, w_d, w_m, w_o)


def _mlp_kernel(x_ref, g_ref, w1_ref, w2_ref, gf_ref, out_ref, *, chunk):
    x = x_ref[...]
    hm = _rms(x, g_ref[...]).astype(BF16)
    acc = x
    for c in range(D_FF // chunk):
        z = jnp.dot(hm, w1_ref[:, c * chunk:(c + 1) * chunk], preferred_element_type=F32)
        a = jnp.square(jnp.maximum(z, 0.0)).astype(BF16)
        acc = acc + jnp.dot(a, w2_ref[c * chunk:(c + 1) * chunk, :], preferred_element_type=F32)
    out_ref[...] = _rms(acc, gf_ref[...])


def _mlp(x1, g_mlp, w1, w2, g_final, tm=512, chunk=1024):
    N = x1.shape[0]
    return pl.pallas_call(
        functools.partial(_mlp_kernel, chunk=chunk),
        grid=(N // tm,),
        in_specs=[pl.BlockSpec((tm, D_MODEL), lambda i: (i, 0)),
                  _const_spec((1, D_MODEL)), _const_spec(w1.shape), _const_spec(w2.shape),
                  _const_spec((1, D_MODEL))],
        out_specs=pl.BlockSpec((tm, D_MODEL), lambda i: (i, 0)),
        out_shape=jax.ShapeDtypeStruct((N, D_MODEL), F32),
        compiler_params=_params("arbitrary"),
        name="mlp",
    )(x1, g_mlp, w1, w2, g_final)


def _gate_pair_weights(w_a, w_x):
    def pairs(w):
        w = w.reshape(N_LRU_BLOCKS // 2, 2, LRU_BLOCK, LRU_BLOCK)
        z = jnp.zeros_like(w[:, 0])
        top = jnp.concatenate([w[:, 0], z], axis=-1)
        bot = jnp.concatenate([z, w[:, 1]], axis=-1)
        return jnp.concatenate([top, bot], axis=-2)
    return jnp.concatenate([pairs(w_a), pairs(w_x)], axis=-1).astype(BF16)


def kernel(x, mem, g_mix, w_in, b_gate, conv_w, conv_b, w_rg_a, b_rg_a, w_rg_x, b_rg_x, lru_lambda, w_lru_out, rel_bias, w_dil_out, g_mem, w_mem_kv, w_mem_out, w_out, g_mlp, w_mlp_in, w_mlp_out, g_final):
    B, S, D = x.shape
    x2 = x.reshape(B * S, D)
    row = lambda v: v.reshape(1, -1)

    qkv0 = 2 * D_RNN
    qm0 = qkv0 + DIL_QKV_WIDTH
    cols = [w_in[:, :qkv0]]
    for g in range(N_DIL_GROUPS):
        part = lambda c: w_in[:, qkv0 + (c * N_DIL_GROUPS + g) * GROUP_WIDTH:
                              qkv0 + (c * N_DIL_GROUPS + g + 1) * GROUP_WIDTH]
        cols += [part(0) * (DIL_HEAD_DIM ** -0.5), part(1), part(2)]
    cols.append(w_in[:, qm0:qm0 + MEM_WIDTH])
    w_a = jnp.concatenate(cols, axis=1).astype(BF16)
    w_g = w_in[:, qm0 + MEM_WIDTH:].astype(BF16)

    mk, mv = _mem_kv(mem, row(g_mem), w_mem_kv.astype(BF16))
    xl, gl, q1, kv1, q4, kv4, q16, kv16, qm = _in_proj(x2, row(g_mix), w_a)
    u = _lru(xl, gl, conv_w, row(conv_b), _gate_pair_weights(w_rg_a, w_rg_x),
             row(b_rg_a), row(b_rg_x), row(lru_lambda), B, S)
    bias = _rel_bias(rel_bias)
    dil_outs = []
    for g, (qg, kvg) in enumerate(((q1, kv1), (q4, kv4), (q16, kv16))):
        dil_outs += _dil_attn(qg, kvg, bias[g * HEADS_PER_GROUP:(g + 1) * HEADS_PER_GROUP], B, S, DIL_GROUPS[g][1])
    x1 = _merge(x2, u, dil_outs, qm, mk, mv, row(g_mix), w_g, row(b_gate),
                w_lru_out.astype(BF16), w_dil_out.astype(BF16), w_mem_out.astype(BF16), w_out.astype(BF16), S)
    y = _mlp(x1, row(g_mlp), w_mlp_in.astype(BF16), w_mlp_out.astype(BF16), row(g_final))
    return y.reshape(B, S, D)
```

```python
import functools
import math

import numpy as np
import jax
import jax.numpy as jnp
from jax import lax
from jax.experimental import pallas as pl
from jax.experimental.pallas import tpu as pltpu

D_MODEL = 1024
N_MEM = 256
MEM_HEADS = 4
MEM_HEAD_DIM = 128
MEM_WIDTH = MEM_HEADS * MEM_HEAD_DIM
D_RNN = 768
LRU_BLOCK = 64
N_LRU_BLOCKS = D_RNN // LRU_BLOCK
CONV_WIDTH = 4
LRU_C = 8.0
DIL_GROUPS = ((128, 1), (512, 4), (2048, 16))
N_DIL_GROUPS = 3
HEADS_PER_GROUP = 4
DIL_HEAD_DIM = 64
N_DIL_HEADS = 12
GROUP_WIDTH = HEADS_PER_GROUP * DIL_HEAD_DIM
DIL_QKV_WIDTH = 3 * N_DIL_HEADS * DIL_HEAD_DIM
SPAN = 128
NUM_BUCKETS = 32
MAX_DISTANCE = 2048
N_BRANCHES = 3
D_FF = 4 * D_MODEL
EPS = 1e-6
NEG = -1e30

LANES = 128
SUBLANES = 8
ATTN_TILE = SPAN * 16
VMEM_LIMIT = 56 * 1024 * 1024

BF16 = jnp.bfloat16
F32 = jnp.float32


def _rms(x32, g):
    return x32 * lax.rsqrt(jnp.mean(x32 * x32, axis=-1, keepdims=True) + EPS) * g


def _const_spec(shape):
    nd = len(shape)
    return pl.BlockSpec(shape, lambda *_: (0,) * nd, pipeline_mode=pl.Buffered(1))


def _params(*sem):
    return pltpu.CompilerParams(dimension_semantics=sem, vmem_limit_bytes=VMEM_LIMIT)


def _mem_kv_kernel(mem_ref, g_ref, w_ref, k_ref, v_ref):
    mn = _rms(mem_ref[0], g_ref[...]).astype(BF16)
    kv = jnp.dot(mn, w_ref[...], preferred_element_type=F32)
    k_ref[0] = kv[:, :MEM_WIDTH].astype(BF16)
    v_ref[0] = kv[:, MEM_WIDTH:].astype(BF16)


def _mem_kv(mem, g_mem, w_kv):
    B = mem.shape[0]
    return pl.pallas_call(
        _mem_kv_kernel,
        grid=(B,),
        in_specs=[pl.BlockSpec((1, N_MEM, D_MODEL), lambda b: (b, 0, 0)),
                  _const_spec((1, D_MODEL)),
                  _const_spec((D_MODEL, 2 * MEM_WIDTH))],
        out_specs=[pl.BlockSpec((1, N_MEM, MEM_WIDTH), lambda b: (b, 0, 0))] * 2,
        out_shape=[jax.ShapeDtypeStruct((B, N_MEM, MEM_WIDTH), BF16)] * 2,
        compiler_params=_params("arbitrary"),
        name="mem_kv",
    )(mem, g_mem, w_kv)


IN_PROJ_OUTS = ((D_RNN, BF16), (D_RNN, BF16),
                (GROUP_WIDTH, BF16), (2 * GROUP_WIDTH, BF16),
                (GROUP_WIDTH, F32), (2 * GROUP_WIDTH, F32),
                (GROUP_WIDTH, F32), (2 * GROUP_WIDTH, F32),
                (MEM_WIDTH, BF16))


def _in_proj_kernel(x_ref, g_ref, w_ref, *out_refs):
    h = _rms(x_ref[...], g_ref[...]).astype(BF16)
    col = 0
    for ref in out_refs:
        n = ref.shape[-1]
        ref[...] = jnp.dot(h, w_ref[:, col:col + n], preferred_element_type=F32).astype(ref.dtype)
        col += n


def _in_proj(x2, g_mix, w_a, tm=512):
    N = x2.shape[0]
    return pl.pallas_call(
        _in_proj_kernel,
        grid=(N // tm,),
        in_specs=[pl.BlockSpec((tm, D_MODEL), lambda i: (i, 0)),
                  _const_spec((1, D_MODEL)),
                  _const_spec(w_a.shape)],
        out_specs=[pl.BlockSpec((tm, n), lambda i: (i, 0)) for n, _ in IN_PROJ_OUTS],
        out_shape=[jax.ShapeDtypeStruct((N, n), dt) for n, dt in IN_PROJ_OUTS],
        compiler_params=_params("arbitrary"),
        name="in_proj",
    )(x2, g_mix, w_a)


def _lru_kernel(xl_ref, gl_ref, cw_ref, cb_ref, wab_ref, ba_ref, bx_ref, lam_ref, u_ref,
                xpad, a_s, b_s, hc):
    tt = xl_ref.shape[0]
    t = pl.program_id(1)

    @pl.when(t == 0)
    def _():
        xpad[0:SUBLANES, :] = jnp.zeros((SUBLANES, D_RNN), F32)
        hc[...] = jnp.zeros_like(hc)

    xl = xl_ref[...].astype(F32)
    xpad[SUBLANES:SUBLANES + tt, :] = xl
    xc = cb_ref[...] + cw_ref[CONV_WIDTH - 1:CONV_WIDTH, :] * xl
    for j in range(CONV_WIDTH - 1):
        lo = SUBLANES - (CONV_WIDTH - 1) + j
        xc = xc + cw_ref[j:j + 1, :] * xpad[lo:lo + tt, :]
    xpad[0:SUBLANES, :] = xl[tt - SUBLANES:tt, :]

    sp = jax.nn.softplus(-lam_ref[...])
    xcb = xc.astype(BF16)
    for c in range(D_RNN // LANES):
        sl = slice(c * LANES, (c + 1) * LANES)
        z = jnp.dot(xcb[:, sl], wab_ref[c], preferred_element_type=F32)
        r = jax.nn.sigmoid(z[:, :LANES] + ba_ref[:, sl])
        i = jax.nn.sigmoid(z[:, LANES:] + bx_ref[:, sl])
        log_a = (-LRU_C) * r * sp[:, sl]
        a = jnp.exp(log_a)
        a_s[:, sl] = a
        b_s[:, sl] = jnp.sqrt(1.0 - a * a) * (i * xc[:, sl])

    row = lax.broadcasted_iota(jnp.int32, (SUBLANES, D_RNN), 0)

    def group(i, hprev):
        r0 = pl.multiple_of(i * SUBLANES, SUBLANES)
        a = a_s[pl.ds(r0, SUBLANES), :]
        b = b_s[pl.ds(r0, SUBLANES), :]
        for s in (1, 2, 4):
            a_sh = jnp.where(row >= s, pltpu.roll(a, s, 0), 1.0)
            b_sh = jnp.where(row >= s, pltpu.roll(b, s, 0), 0.0)
            b = a * b_sh + b
            a = a * a_sh
        h = b + a * hprev
        b_s[pl.ds(r0, SUBLANES), :] = h
        return jnp.broadcast_to(h[SUBLANES - 1:SUBLANES, :], (SUBLANES, D_RNN))

    hc[...] = lax.fori_loop(0, tt // SUBLANES, group, hc[...])

    g = gl_ref[...].astype(F32)
    gelu = 0.5 * g * (1.0 + jnp.tanh(math.sqrt(2.0 / math.pi) * (g + 0.044715 * (g * g * g))))
    u_ref[...] = (gelu * b_s[...]).astype(BF16)


def _lru(xl, gl, conv_w, conv_b, wab, b_a, b_x, lam, B, S, tt=512):
    nt = S // tt
    row_spec = pl.BlockSpec((tt, D_RNN), lambda b, t: (b * nt + t, 0))
    return pl.pallas_call(
        _lru_kernel,
        grid=(B, nt),
        in_specs=[row_spec, row_spec,
                  _const_spec((CONV_WIDTH, D_RNN)), _const_spec((1, D_RNN)),
                  _const_spec(wab.shape),
                  _const_spec((1, D_RNN)), _const_spec((1, D_RNN)), _const_spec((1, D_RNN))],
        out_specs=row_spec,
        out_shape=jax.ShapeDtypeStruct((B * S, D_RNN), BF16),
        scratch_shapes=[pltpu.VMEM((tt + SUBLANES, D_RNN), F32),
                        pltpu.VMEM((tt, D_RNN), F32),
                        pltpu.VMEM((tt, D_RNN), F32),
                        pltpu.VMEM((SUBLANES, D_RNN), F32)],
        compiler_params=_params("arbitrary", "arbitrary"),
        name="lru",
    )(xl, gl, conv_w, conv_b, wab, b_a, b_x, lam)


def _bucket_tables():
    kj = np.arange(2 * SPAN)[:, None]
    qi = np.arange(SPAN)[None, :]
    off = qi + SPAN - kj
    valid = (off >= 0) & (off <= SPAN)
    max_exact = NUM_BUCKETS // 2
    tabs = []
    for _, dil in DIL_GROUPS:
        dist = np.maximum(off, 0) * dil
        df = np.maximum(dist, 1).astype(np.float32)
        large = max_exact + (np.log(df / max_exact) / math.log(MAX_DISTANCE / max_exact)
                             * (NUM_BUCKETS - max_exact)).astype(np.int32)
        large = np.minimum(large, NUM_BUCKETS - 1)
        bucket = np.where(dist < max_exact, dist, large)
        tabs.append(np.where(valid, bucket, -1).astype(np.int32))
    return np.stack(tabs)


def _rel_bias_kernel(tab_ref, bucket_ref, out_ref):
    hd = pl.program_id(0)
    bk = bucket_ref[0]
    acc = jnp.full(bk.shape, NEG, F32)
    for b in range(NUM_BUCKETS):
        acc = jnp.where(bk == b, tab_ref[b * N_DIL_HEADS + hd], acc)
    out_ref[0] = acc


def _rel_bias(rel_bias):
    buckets = jnp.asarray(_bucket_tables())
    return pl.pallas_call(
        _rel_bias_kernel,
        grid=(N_DIL_HEADS,),
        in_specs=[pl.BlockSpec(memory_space=pltpu.SMEM),
                  pl.BlockSpec((1, 2 * SPAN, SPAN), lambda h: (h // HEADS_PER_GROUP, 0, 0))],
        out_specs=pl.BlockSpec((1, 2 * SPAN, SPAN), lambda h: (h, 0, 0)),
        out_shape=jax.ShapeDtypeStruct((N_DIL_HEADS, 2 * SPAN, SPAN), F32),
        compiler_params=_params("arbitrary"),
        name="rel_bias",
    )(rel_bias.reshape(-1), buckets)


def _dil_attn_kernel(q0_ref, q1_ref, k0_ref, k1_ref, v0_ref, v1_ref, kp0_ref, kp1_ref, vp0_ref, vp1_ref,
                     bias_ref, o0_ref, o1_ref, l0_ref, l1_ref, *, dil):
    first_tile = pl.program_id(1) == 0
    lane = lax.broadcasted_iota(jnp.int32, (SPAN, LANES), 1)
    key_row = lax.broadcasted_iota(jnp.int32, (2 * SPAN, SPAN), 0)
    n_blk = ATTN_TILE // (dil * SPAN)

    def rows(ref, start):
        if dil == 1:
            return ref[start:start + SPAN, :]
        return ref[pl.ds(start, SPAN, stride=dil), :].astype(BF16)

    def put(ref, start, val):
        if dil == 1:
            ref[start:start + SPAN, :] = val
        else:
            ref[pl.ds(start, SPAN, stride=dil), :] = val

    def block(q2, kcat, vcat, pair, mask_prev):
        qq = jnp.concatenate([jnp.where(lane < DIL_HEAD_DIM, q2, 0),
                              jnp.where(lane >= DIL_HEAD_DIM, q2, 0)], axis=0)
        s_all = lax.dot_general(kcat, qq, (((1,), (1,)), ((), ())), preferred_element_type=F32)
        o_t, lse_t = [], []
        for e in range(2):
            s = s_all[:, e * SPAN:(e + 1) * SPAN] + bias_ref[2 * pair + e]
            if mask_prev:
                s = jnp.where(jnp.logical_and(first_tile, key_row < SPAN), NEG, s)
            m = jnp.max(s, axis=0, keepdims=True)
            p = jnp.exp(s - m)
            den = jnp.sum(p, axis=0, keepdims=True)
            ov = lax.dot_general(vcat, p.astype(BF16), (((0,), (0,)), ((), ())), preferred_element_type=F32)
            o_t.append(ov[e * DIL_HEAD_DIM:(e + 1) * DIL_HEAD_DIM, :] / den)
            lse_t.append(jnp.broadcast_to(m + jnp.log(den), (DIL_HEAD_DIM, SPAN)))
        return jnp.concatenate(o_t, axis=0).T, jnp.concatenate(lse_t, axis=0).T

    pairs = ((q0_ref, k0_ref, v0_ref, kp0_ref, vp0_ref, o0_ref, l0_ref),
             (q1_ref, k1_ref, v1_ref, kp1_ref, vp1_ref, o1_ref, l1_ref))
    for r in range(dil):
        for pair, (q_ref, k_ref, v_ref, kp_ref, vp_ref, o_ref, l_ref) in enumerate(pairs):
            k_prev, v_prev = rows(kp_ref, r), rows(vp_ref, r)
            for n in range(n_blk):
                start = r + dil * SPAN * n
                k_cur, v_cur = rows(k_ref, start), rows(v_ref, start)
                o, lse = block(rows(q_ref, start), jnp.concatenate([k_prev, k_cur], axis=0),
                               jnp.concatenate([v_prev, v_cur], axis=0), pair, n == 0)
                put(o_ref, start, o)
                put(l_ref, start, lse)
                k_prev, v_prev = k_cur, v_cur


def _dil_attn(q, kv, bias_g, B, S, dil):
    nt = S // ATTN_TILE
    prev_rows = dil * SPAN
    per_tile = ATTN_TILE // prev_rows

    def cur(col):
        return pl.BlockSpec((ATTN_TILE, LANES), lambda b, t: (b * nt + t, col))

    def prev(col):
        return pl.BlockSpec((prev_rows, LANES), lambda b, t: (jnp.maximum((b * nt + t) * per_tile - 1, 0), col))

    return pl.pallas_call(
        functools.partial(_dil_attn_kernel, dil=dil),
        grid=(B, nt),
        in_specs=[cur(0), cur(1), cur(0), cur(1), cur(2), cur(3), prev(0), prev(1), prev(2), prev(3),
                  _const_spec(bias_g.shape)],
        out_specs=[cur(0)] * 4,
        out_shape=[jax.ShapeDtypeStruct((B * S, LANES), F32)] * 4,
        compiler_params=_params("arbitrary", "arbitrary"),
        name=f"dil_attn_d{dil}",
    )(q, q, kv, kv, kv, kv, kv, kv, kv, kv, bias_g)


def _merge_kernel(x_ref, u_ref, *rest):
    dil_refs, (qm_ref, mk_ref, mv_ref, g_ref, wg_ref, bg_ref, wl_ref, wd_ref, wm_ref, wo_ref, out_ref) = \
        rest[:4 * N_DIL_GROUPS], rest[4 * N_DIL_GROUPS:]
    x = x_ref[...]
    h = _rms(x, g_ref[...]).astype(BF16)

    def gate(j):
        z = jnp.dot(h, wg_ref[:, j * D_MODEL:(j + 1) * D_MODEL], preferred_element_type=F32)
        return jax.nn.sigmoid(z + bg_ref[:, j * D_MODEL:(j + 1) * D_MODEL])

    merged = gate(0) * jnp.dot(u_ref[...], wl_ref[...], preferred_element_type=F32)

    o_pairs = []
    for pair in range(2):
        o_g = [dil_refs[4 * g + pair][...] for g in range(N_DIL_GROUPS)]
        l_g = [dil_refs[4 * g + 2 + pair][...] for g in range(N_DIL_GROUPS)]
        mx = jnp.maximum(jnp.maximum(l_g[0], l_g[1]), l_g[2])
        e_g = [jnp.exp(l - mx) for l in l_g]
        o_pairs.append((e_g[0] * o_g[0] + e_g[1] * o_g[1] + e_g[2] * o_g[2]) / (e_g[0] + e_g[1] + e_g[2]))
    o_dil = jnp.concatenate(o_pairs, axis=-1).astype(BF16)
    merged = merged + gate(1) * jnp.dot(o_dil, wd_ref[...], preferred_element_type=F32)

    om = []
    for hd in range(MEM_HEADS):
        sl = slice(hd * MEM_HEAD_DIM, (hd + 1) * MEM_HEAD_DIM)
        s = lax.dot_general(qm_ref[:, sl], mk_ref[0, :, sl], (((1,), (1,)), ((), ())),
                            preferred_element_type=F32) * (MEM_HEAD_DIM ** -0.5)
        p = jnp.exp(s - jnp.max(s, axis=-1, keepdims=True))
        den = jnp.sum(p, axis=-1, keepdims=True)
        om.append(jnp.dot(p.astype(BF16), mv_ref[0, :, sl], preferred_element_type=F32) / den)
    om = jnp.concatenate(om, axis=-1).astype(BF16)
    merged = merged + gate(2) * jnp.dot(om, wm_ref[...], preferred_element_type=F32)

    out_ref[...] = x + jnp.dot(merged.astype(BF16), wo_ref[...], preferred_element_type=F32)


def _merge(x2, u, dil_outs, qm, mk, mv, g_mix, w_g, b_gate, w_l, w_d, w_m, w_o, S, tm=512):
    N = x2.shape[0]
    nt = S // tm
    def rows(n):
        return pl.BlockSpec((tm, n), lambda i: (i, 0))
    mem_spec = pl.BlockSpec((1, N_MEM, MEM_WIDTH), lambda i: (i // nt, 0, 0))
    return pl.pallas_call(
        _merge_kernel,
        grid=(N // tm,),
        in_specs=[rows(D_MODEL), rows(D_RNN)] + [rows(LANES)] * len(dil_outs) + [rows(MEM_WIDTH), mem_spec, mem_spec,
                  _const_spec((1, D_MODEL)), _const_spec(w_g.shape), _const_spec((1, N_BRANCHES * D_MODEL)),
                  _const_spec(w_l.shape), _const_spec(w_d.shape), _const_spec(w_m.shape), _const_spec(w_o.shape)],
        out_specs=rows(D_MODEL),
        out_shape=jax.ShapeDtypeStruct((N, D_MODEL), F32),
        compiler_params=_params("arbitrary"),
        name="merge",
    )(x2, u, *dil_outs, qm, mk, mv, g_mix, w_g, b_gate, w_l, w_d, w_m, w_o)


def _mlp_kernel(x_ref, g_ref, w1_ref, w2_ref, gf_ref, out_ref, *, chunk):
    x = x_ref[...]
    hm = _rms(x, g_ref[...]).astype(BF16)
    acc = x
    for c in range(D_FF // chunk):
        z = jnp.dot(hm, w1_ref[:, c * chunk:(c + 1) * chunk], preferred_element_type=F32)
        a = jnp.square(jnp.maximum(z, 0.0)).astype(BF16)
        acc = acc + jnp.dot(a, w2_ref[c * chunk:(c + 1) * chunk, :], preferred_element_type=F32)
    out_ref[...] = _rms(acc, gf_ref[...])


def _mlp(x1, g_mlp, w1, w2, g_final, tm=512, chunk=1024):
    N = x1.shape[0]
    return pl.pallas_call(
        functools.partial(_mlp_kernel, chunk=chunk),
        grid=(N // tm,),
        in_specs=[pl.BlockSpec((tm, D_MODEL), lambda i: (i, 0)),
                  _const_spec((1, D_MODEL)), _const_spec(w1.shape), _const_spec(w2.shape),
                  _const_spec((1, D_MODEL))],
        out_specs=pl.BlockSpec((tm, D_MODEL), lambda i: (i, 0)),
        out_shape=jax.ShapeDtypeStruct((N, D_MODEL), F32),
        compiler_params=_params("arbitrary"),
        name="mlp",
    )(x1, g_mlp, w1, w2, g_final)


def _gate_pair_weights(w_a, w_x):
    def pairs(w):
        w = w.reshape(N_LRU_BLOCKS // 2, 2, LRU_BLOCK, LRU_BLOCK)
        z = jnp.zeros_like(w[:, 0])
        top = jnp.concatenate([w[:, 0], z], axis=-1)
        bot = jnp.concatenate([z, w[:, 1]], axis=-1)
        return jnp.concatenate([top, bot], axis=-2)
    return jnp.concatenate([pairs(w_a), pairs(w_x)], axis=-1).astype(BF16)


def kernel(x, mem, g_mix, w_in, b_gate, conv_w, conv_b, w_rg_a, b_rg_a, w_rg_x, b_rg_x, lru_lambda, w_lru_out, rel_bias, w_dil_out, g_mem, w_mem_kv, w_mem_out, w_out, g_mlp, w_mlp_in, w_mlp_out, g_final):
    B, S, D = x.shape
    x2 = x.reshape(B * S, D)
    row = lambda v: v.reshape(1, -1)

    qkv0 = 2 * D_RNN
    qm0 = qkv0 + DIL_QKV_WIDTH
    cols = [w_in[:, :qkv0]]
    for g in range(N_DIL_GROUPS):
        part = lambda c: w_in[:, qkv0 + (c * N_DIL_GROUPS + g) * GROUP_WIDTH:
                              qkv0 + (c * N_DIL_GROUPS + g + 1) * GROUP_WIDTH]
        cols += [part(0) * (DIL_HEAD_DIM ** -0.5), part(1), part(2)]
    cols.append(w_in[:, qm0:qm0 + MEM_WIDTH])
    w_a = jnp.concatenate(cols, axis=1).astype(BF16)
    w_g = w_in[:, qm0 + MEM_WIDTH:].astype(BF16)

    mk, mv = _mem_kv(mem, row(g_mem), w_mem_kv.astype(BF16))
    xl, gl, q1, kv1, q4, kv4, q16, kv16, qm = _in_proj(x2, row(g_mix), w_a)
    u = _lru(xl, gl, conv_w, row(conv_b), _gate_pair_weights(w_rg_a, w_rg_x),
             row(b_rg_a), row(b_rg_x), row(lru_lambda), B, S)
    bias = _rel_bias(rel_bias)
    dil_outs = []
    for g, (qg, kvg) in enumerate(((q1, kv1), (q4, kv4), (q16, kv16))):
        dil_outs += _dil_attn(qg, kvg, bias[g * HEADS_PER_GROUP:(g + 1) * HEADS_PER_GROUP], B, S, DIL_GROUPS[g][1])
    x1 = _merge(x2, u, dil_outs, qm, mk, mv, row(g_mix), w_g, row(b_gate),
                w_lru_out.astype(BF16), w_dil_out.astype(BF16), w_mem_out.astype(BF16), w_out.astype(BF16), S)
    y = _mlp(x1, row(g_mlp), w_mlp_in.astype(BF16), w_mlp_out.astype(BF16), row(g_final))
    return y.reshape(B, S, D)
```

```python
import functools
import math

import numpy as np
import jax
import jax.numpy as jnp
from jax import lax
from jax.experimental import pallas as pl
from jax.experimental.pallas import tpu as pltpu

D_MODEL = 1024
N_MEM = 256
MEM_HEADS = 4
MEM_HEAD_DIM = 128
MEM_WIDTH = MEM_HEADS * MEM_HEAD_DIM
D_RNN = 768
LRU_BLOCK = 64
N_LRU_BLOCKS = D_RNN // LRU_BLOCK
CONV_WIDTH = 4
LRU_C = 8.0
DIL_GROUPS = ((128, 1), (512, 4), (2048, 16))
N_DIL_GROUPS = 3
HEADS_PER_GROUP = 4
DIL_HEAD_DIM = 64
N_DIL_HEADS = 12
GROUP_WIDTH = HEADS_PER_GROUP * DIL_HEAD_DIM
DIL_QKV_WIDTH = 3 * N_DIL_HEADS * DIL_HEAD_DIM
SPAN = 128
NUM_BUCKETS = 32
MAX_DISTANCE = 2048
N_BRANCHES = 3
D_FF = 4 * D_MODEL
EPS = 1e-6
NEG = -1e30

LANES = 128
SUBLANES = 8
ATTN_TILE = SPAN * 16
VMEM_LIMIT = 56 * 1024 * 1024

BF16 = jnp.bfloat16
F32 = jnp.float32


def _rms(x32, g):
    return x32 * lax.rsqrt(jnp.mean(x32 * x32, axis=-1, keepdims=True) + EPS) * g


def _const_spec(shape):
    nd = len(shape)
    return pl.BlockSpec(shape, lambda *_: (0,) * nd, pipeline_mode=pl.Buffered(1))


def _params(*sem):
    return pltpu.CompilerParams(dimension_semantics=sem, vmem_limit_bytes=VMEM_LIMIT)


def _mem_kv_kernel(mem_ref, g_ref, w_ref, k_ref, v_ref):
    mn = _rms(mem_ref[0], g_ref[...]).astype(BF16)
    kv = jnp.dot(mn, w_ref[...], preferred_element_type=F32)
    k_ref[0] = kv[:, :MEM_WIDTH].astype(BF16)
    v_ref[0] = kv[:, MEM_WIDTH:].astype(BF16)


def _mem_kv(mem, g_mem, w_kv):
    B = mem.shape[0]
    return pl.pallas_call(
        _mem_kv_kernel,
        grid=(B,),
        in_specs=[pl.BlockSpec((1, N_MEM, D_MODEL), lambda b: (b, 0, 0)),
                  _const_spec((1, D_MODEL)),
                  _const_spec((D_MODEL, 2 * MEM_WIDTH))],
        out_specs=[pl.BlockSpec((1, N_MEM, MEM_WIDTH), lambda b: (b, 0, 0))] * 2,
        out_shape=[jax.ShapeDtypeStruct((B, N_MEM, MEM_WIDTH), BF16)] * 2,
        compiler_params=_params("arbitrary"),
        name="mem_kv",
    )(mem, g_mem, w_kv)


IN_PROJ_OUTS = ((GROUP_WIDTH, BF16), (2 * GROUP_WIDTH, BF16),
                (GROUP_WIDTH, F32), (2 * GROUP_WIDTH, F32),
                (GROUP_WIDTH, F32), (2 * GROUP_WIDTH, F32),
                (MEM_WIDTH, BF16))


def _in_proj_lru_kernel(x_ref, g_ref, w_ref, cw_ref, cb_ref, wab_ref, ba_ref, bx_ref, lam_ref, *refs,
                        tiles_per_seq):
    n_out = len(IN_PROJ_OUTS)
    out_refs, u_ref = refs[:n_out], refs[n_out]
    xl_s, gl_s, h_s, xtail, hc = refs[n_out + 1:]
    tt = x_ref.shape[0]
    ng = tt // SUBLANES
    i = pl.program_id(0)

    @pl.when(i == 0)
    def _():
        xl_s[...] = jnp.zeros_like(xl_s)
        gl_s[...] = jnp.zeros_like(gl_s)
        xtail[...] = jnp.zeros_like(xtail)
        hc[...] = jnp.zeros_like(hc)

    h = _rms(x_ref[...], g_ref[...]).astype(BF16)
    cur = i % 2
    prev = 1 - cur
    fresh = lax.rem(i + tiles_per_seq - 1, tiles_per_seq) == 0
    grow = lax.broadcasted_iota(jnp.int32, (ng, LANES), 0)

    def project(lo, hi):
        col = 0
        dests = [(None, D_RNN), (gl_s.at[cur], D_RNN)] + [(r, r.shape[-1]) for r in out_refs]
        for ref, n in dests:
            a, b = max(lo, col), min(hi, col + n)
            if a < b:
                part = jnp.dot(h, w_ref[:, a:b], preferred_element_type=F32)
                if ref is None:
                    for k in range(a // LANES, b // LANES):
                        xl_s[cur, k] = part[:, k * LANES - a:(k + 1) * LANES - a]
                else:
                    ref[:, a - col:b - col] = part.astype(ref.dtype)
            col += n

    def shift_groups(v, row0):
        return jnp.where(grow == 0, row0, pltpu.roll(v, 1, 0))

    def lru_chunk(c):
        sl = slice(c * LANES, (c + 1) * LANES)
        step = {t: xl_s[prev, c, pl.ds(t, ng, stride=SUBLANES), :] for t in range(SUBLANES)}
        tail = jnp.where(fresh, 0.0, xtail[c])
        xtail[c] = xl_s[prev, c, tt - SUBLANES:tt, :]
        for k in range(1, CONV_WIDTH):
            step[-k] = shift_groups(step[SUBLANES - k], tail[SUBLANES - k:SUBLANES - k + 1, :])
        xc = []
        for t in range(SUBLANES):
            acc = cb_ref[:, sl]
            for j in range(CONV_WIDTH):
                acc = acc + cw_ref[j:j + 1, sl] * step[t - (CONV_WIDTH - 1) + j]
            xc.append(acc)
        xc = jnp.concatenate(xc, axis=0)

        z = jnp.dot(xc.astype(BF16), wab_ref[c], preferred_element_type=F32)
        r = jax.nn.sigmoid(z[:, :LANES] + ba_ref[:, sl])
        gi = jax.nn.sigmoid(z[:, LANES:] + bx_ref[:, sl])
        a = jnp.exp((-LRU_C) * r * jax.nn.softplus(-lam_ref[:, sl]))
        b = jnp.sqrt(1.0 - a * a) * (gi * xc)

        hs, ps = [b[0:ng]], [a[0:ng]]
        for t in range(1, SUBLANES):
            at, bt = a[t * ng:(t + 1) * ng], b[t * ng:(t + 1) * ng]
            hs.append(at * hs[-1] + bt)
            ps.append(at * ps[-1])
        q, e = ps[-1], hs[-1]
        s = 1
        while s < ng:
            q_sh = jnp.where(grow >= s, pltpu.roll(q, s, 0), 1.0)
            e_sh = jnp.where(grow >= s, pltpu.roll(e, s, 0), 0.0)
            e = q * e_sh + e
            q = q * q_sh
            s *= 2
        h0 = jnp.where(fresh, 0.0, hc[c, SUBLANES - 1:SUBLANES, :])
        end = e + q * h0
        hc[c] = end[ng - SUBLANES:ng]
        carry = shift_groups(end, h0)
        for t in range(SUBLANES):
            h_s[c, pl.ds(t, ng, stride=SUBLANES), :] = hs[t] + ps[t] * carry

        g = gl_s[prev, :, sl]
        gelu = 0.5 * g * (1.0 + jnp.tanh(math.sqrt(2.0 / math.pi) * (g + 0.044715 * (g * g * g))))
        u_ref[:, sl] = (gelu * h_s[c]).astype(BF16)

    n_chunks = D_RNN // LANES
    width = w_ref.shape[1]
    piece = -(-width // (n_chunks * LANES)) * LANES
    for c in range(n_chunks):
        project(c * piece, min((c + 1) * piece, width))
        lru_chunk(c)


def _in_proj_lru(x2, g_mix, w_a, conv_w, conv_b, wab, b_a, b_x, lam, S, tm=512):
    N = x2.shape[0]
    nt = N // tm
    n_chunks = D_RNN // LANES
    def tile(n):
        return pl.BlockSpec((tm, n), lambda i: (jnp.minimum(i, nt - 1), 0))
    return pl.pallas_call(
        functools.partial(_in_proj_lru_kernel, tiles_per_seq=S // tm),
        grid=(nt + 1,),
        in_specs=[tile(D_MODEL), _const_spec((1, D_MODEL)), _const_spec(w_a.shape),
                  _const_spec((CONV_WIDTH, D_RNN)), _const_spec((1, D_RNN)), _const_spec(wab.shape),
                  _const_spec((1, D_RNN)), _const_spec((1, D_RNN)), _const_spec((1, D_RNN))],
        out_specs=[tile(n) for n, _ in IN_PROJ_OUTS]
                  + [pl.BlockSpec((tm, D_RNN), lambda i: (jnp.maximum(i - 1, 0), 0))],
        out_shape=[jax.ShapeDtypeStruct((N, n), dt) for n, dt in IN_PROJ_OUTS]
                  + [jax.ShapeDtypeStruct((N, D_RNN), BF16)],
        scratch_shapes=[pltpu.VMEM((2, n_chunks, tm, LANES), F32),
                        pltpu.VMEM((2, tm, D_RNN), F32),
                        pltpu.VMEM((n_chunks, tm, LANES), F32),
                        pltpu.VMEM((n_chunks, SUBLANES, LANES), F32),
                        pltpu.VMEM((n_chunks, SUBLANES, LANES), F32)],
        compiler_params=_params("arbitrary"),
        name="in_proj_lru",
    )(x2, g_mix, w_a, conv_w, conv_b, wab, b_a, b_x, lam)


def _bucket_tables():
    kj = np.arange(2 * SPAN)[:, None]
    qi = np.arange(SPAN)[None, :]
    off = qi + SPAN - kj
    valid = (off >= 0) & (off <= SPAN)
    max_exact = NUM_BUCKETS // 2
    tabs = []
    for _, dil in DIL_GROUPS:
        dist = np.maximum(off, 0) * dil
        df = np.maximum(dist, 1).astype(np.float32)
        large = max_exact + (np.log(df / max_exact) / math.log(MAX_DISTANCE / max_exact)
                             * (NUM_BUCKETS - max_exact)).astype(np.int32)
        large = np.minimum(large, NUM_BUCKETS - 1)
        bucket = np.where(dist < max_exact, dist, large)
        tabs.append(np.where(valid, bucket, -1).astype(np.int32))
    return np.stack(tabs)


def _rel_bias_kernel(tab_ref, bucket_ref, out_ref):
    hd = pl.program_id(0)
    bk = bucket_ref[0]
    acc = jnp.full(bk.shape, NEG, F32)
    for b in range(NUM_BUCKETS):
        acc = jnp.where(bk == b, tab_ref[b * N_DIL_HEADS + hd], acc)
    out_ref[0] = acc


def _rel_bias(rel_bias):
    buckets = jnp.asarray(_bucket_tables())
    return pl.pallas_call(
        _rel_bias_kernel,
        grid=(N_DIL_HEADS,),
        in_specs=[pl.BlockSpec(memory_space=pltpu.SMEM),
                  pl.BlockSpec((1, 2 * SPAN, SPAN), lambda h: (h // HEADS_PER_GROUP, 0, 0))],
        out_specs=pl.BlockSpec((1, 2 * SPAN, SPAN), lambda h: (h, 0, 0)),
        out_shape=jax.ShapeDtypeStruct((N_DIL_HEADS, 2 * SPAN, SPAN), F32),
        compiler_params=_params("arbitrary"),
        name="rel_bias",
    )(rel_bias.reshape(-1), buckets)


def _dil_attn_kernel(q0_ref, q1_ref, k0_ref, k1_ref, v0_ref, v1_ref, kp0_ref, kp1_ref, vp0_ref, vp1_ref,
                     bias_ref, o0_ref, o1_ref, l0_ref, l1_ref, *, dil):
    first_tile = pl.program_id(1) == 0
    lane = lax.broadcasted_iota(jnp.int32, (SPAN, LANES), 1)
    key_row = lax.broadcasted_iota(jnp.int32, (2 * SPAN, SPAN), 0)
    n_blk = ATTN_TILE // (dil * SPAN)

    def rows(ref, start):
        if dil == 1:
            return ref[start:start + SPAN, :]
        return ref[pl.ds(start, SPAN, stride=dil), :].astype(BF16)

    def put(ref, start, val):
        if dil == 1:
            ref[start:start + SPAN, :] = val
        else:
            ref[pl.ds(start, SPAN, stride=dil), :] = val

    def block(q2, kcat, vcat, pair, mask_prev):
        qq = jnp.concatenate([jnp.where(lane < DIL_HEAD_DIM, q2, 0),
                              jnp.where(lane >= DIL_HEAD_DIM, q2, 0)], axis=0)
        s_all = lax.dot_general(kcat, qq, (((1,), (1,)), ((), ())), preferred_element_type=F32)
        o_t, lse_t = [], []
        for e in range(2):
            s = s_all[:, e * SPAN:(e + 1) * SPAN] + bias_ref[2 * pair + e]
            if mask_prev:
                s = jnp.where(jnp.logical_and(first_tile, key_row < SPAN), NEG, s)
            m = jnp.max(s, axis=0, keepdims=True)
            p = jnp.exp(s - m)
            den = jnp.sum(p, axis=0, keepdims=True)
            ov = lax.dot_general(vcat, p.astype(BF16), (((0,), (0,)), ((), ())), preferred_element_type=F32)
            o_t.append(ov[e * DIL_HEAD_DIM:(e + 1) * DIL_HEAD_DIM, :] / den)
            lse_t.append(jnp.broadcast_to(m + jnp.log(den), (DIL_HEAD_DIM, SPAN)))
        return jnp.concatenate(o_t, axis=0).T, jnp.concatenate(lse_t, axis=0).T

    pairs = ((q0_ref, k0_ref, v0_ref, kp0_ref, vp0_ref, o0_ref, l0_ref),
             (q1_ref, k1_ref, v1_ref, kp1_ref, vp1_ref, o1_ref, l1_ref))
    for r in range(dil):
        for pair, (q_ref, k_ref, v_ref, kp_ref, vp_ref, o_ref, l_ref) in enumerate(pairs):
            k_prev, v_prev = rows(kp_ref, r), rows(vp_ref, r)
            for n in range(n_blk):
                start = r + dil * SPAN * n
                k_cur, v_cur = rows(k_ref, start), rows(v_ref, start)
                o, lse = block(rows(q_ref, start), jnp.concatenate([k_prev, k_cur], axis=0),
                               jnp.concatenate([v_prev, v_cur], axis=0), pair, n == 0)
                put(o_ref, start, o)
                put(l_ref, start, lse)
                k_prev, v_prev = k_cur, v_cur


def _dil_attn(q, kv, bias_g, B, S, dil):
    nt = S // ATTN_TILE
    prev_rows = dil * SPAN
    per_tile = ATTN_TILE // prev_rows

    def cur(col):
        return pl.BlockSpec((ATTN_TILE, LANES), lambda b, t: (b * nt + t, col))

    def prev(col):
        return pl.BlockSpec((prev_rows, LANES), lambda b, t: (jnp.maximum((b * nt + t) * per_tile - 1, 0), col))

    return pl.pallas_call(
        functools.partial(_dil_attn_kernel, dil=dil),
        grid=(B, nt),
        in_specs=[cur(0), cur(1), cur(0), cur(1), cur(2), cur(3), prev(0), prev(1), prev(2), prev(3),
                  _const_spec(bias_g.shape)],
        out_specs=[cur(0)] * 4,
        out_shape=[jax.ShapeDtypeStruct((B * S, LANES), F32)] * 4,
        compiler_params=_params("arbitrary", "arbitrary"),
        name=f"dil_attn_d{dil}",
    )(q, q, kv, kv, kv, kv, kv, kv, kv, kv, bias_g)


def _merge_kernel(x_ref, u_ref, *rest):
    dil_refs, (qm_ref, mk_ref, mv_ref, g_ref, wg_ref, bg_ref, wl_ref, wd_ref, wm_ref, wo_ref, out_ref) = \
        rest[:4 * N_DIL_GROUPS], rest[4 * N_DIL_GROUPS:]
    x = x_ref[...]
    h = _rms(x, g_ref[...]).astype(BF16)

    def gate(j):
        z = jnp.dot(h, wg_ref[:, j * D_MODEL:(j + 1) * D_MODEL], preferred_element_type=F32)
        return jax.nn.sigmoid(z + bg_ref[:, j * D_MODEL:(j + 1) * D_MODEL])

    merged = gate(0) * jnp.dot(u_ref[...], wl_ref[...], preferred_element_type=F32)

    o_pairs = []
    for pair in range(2):
        o_g = [dil_refs[4 * g + pair][...] for g in range(N_DIL_GROUPS)]
        l_g = [dil_refs[4 * g + 2 + pair][...] for g in range(N_DIL_GROUPS)]
        mx = jnp.maximum(jnp.maximum(l_g[0], l_g[1]), l_g[2])
        e_g = [jnp.exp(l - mx) for l in l_g]
        o_pairs.append((e_g[0] * o_g[0] + e_g[1] * o_g[1] + e_g[2] * o_g[2]) / (e_g[0] + e_g[1] + e_g[2]))
    o_dil = jnp.concatenate(o_pairs, axis=-1).astype(BF16)
    merged = merged + gate(1) * jnp.dot(o_dil, wd_ref[...], preferred_element_type=F32)

    om = []
    for hd in range(MEM_HEADS):
        sl = slice(hd * MEM_HEAD_DIM, (hd + 1) * MEM_HEAD_DIM)
        s = lax.dot_general(qm_ref[:, sl], mk_ref[0, :, sl], (((1,), (1,)), ((), ())),
                            preferred_element_type=F32) * (MEM_HEAD_DIM ** -0.5)
        p = jnp.exp(s - jnp.max(s, axis=-1, keepdims=True))
        den = jnp.sum(p, axis=-1, keepdims=True)
        om.append(jnp.dot(p.astype(BF16), mv_ref[0, :, sl], preferred_element_type=F32) / den)
    om = jnp.concatenate(om, axis=-1).astype(BF16)
    merged = merged + gate(2) * jnp.dot(om, wm_ref[...], preferred_element_type=F32)

    out_ref[...] = x + jnp.dot(merged.astype(BF16), wo_ref[...], preferred_element_type=F32)


def _merge(x2, u, dil_outs, qm, mk, mv, g_mix, w_g, b_gate, w_l, w_d, w_m, w_o, S, tm=512):
    N = x2.shape[0]
    nt = S // tm
    def rows(n):
        return pl.BlockSpec((tm, n), lambda i: (i, 0))
    mem_spec = pl.BlockSpec((1, N_MEM, MEM_WIDTH), lambda i: (i // nt, 0, 0))
    return pl.pallas_call(
        _merge_kernel,
        grid=(N // tm,),
        in_specs=[rows(D_MODEL), rows(D_RNN)] + [rows(LANES)] * len(dil_outs) + [rows(MEM_WIDTH), mem_spec, mem_spec,
                  _const_spec((1, D_MODEL)), _const_spec(w_g.shape), _const_spec((1, N_BRANCHES * D_MODEL)),
                  _const_spec(w_l.shape), _const_spec(w_d.shape), _const_spec(w_m.shape), _const_spec(w_o.shape)],
        out_specs=rows(D_MODEL),
        out_shape=jax.ShapeDtypeStruct((N, D_MODEL), F32),
        compiler_params=_params("arbitrary"),
        name="merge",
    )(x2, u, *dil_outs, qm, mk, mv, g_mix, w_g, b_gate, w_l, w_d, w_m, w_o)


def _mlp_kernel(x_ref, g_ref, w1_ref, w2_ref, gf_ref, out_ref, *, chunk):
    x = x_ref[...]
    hm = _rms(x, g_ref[...]).astype(BF16)
    acc = x
    for c in range(D_FF // chunk):
        z = jnp.dot(hm, w1_ref[:, c * chunk:(c + 1) * chunk], preferred_element_type=F32)
        a = jnp.square(jnp.maximum(z, 0.0)).astype(BF16)
        acc = acc + jnp.dot(a, w2_ref[c * chunk:(c + 1) * chunk, :], preferred_element_type=F32)
    out_ref[...] = _rms(acc, gf_ref[...])


def _mlp(x1, g_mlp, w1, w2, g_final, tm=512, chunk=1024):
    N = x1.shape[0]
    return pl.pallas_call(
        functools.partial(_mlp_kernel, chunk=chunk),
        grid=(N // tm,),
        in_specs=[pl.BlockSpec((tm, D_MODEL), lambda i: (i, 0)),
                  _const_spec((1, D_MODEL)), _const_spec(w1.shape), _const_spec(w2.shape),
                  _const_spec((1, D_MODEL))],
        out_specs=pl.BlockSpec((tm, D_MODEL), lambda i: (i, 0)),
        out_shape=jax.ShapeDtypeStruct((N, D_MODEL), F32),
        compiler_params=_params("arbitrary"),
        name="mlp",
    )(x1, g_mlp, w1, w2, g_final)


def _gate_pair_weights(w_a, w_x):
    def pairs(w):
        w = w.reshape(N_LRU_BLOCKS // 2, 2, LRU_BLOCK, LRU_BLOCK)
        z = jnp.zeros_like(w[:, 0])
        top = jnp.concatenate([w[:, 0], z], axis=-1)
        bot = jnp.concatenate([z, w[:, 1]], axis=-1)
        return jnp.concatenate([top, bot], axis=-2)
    return jnp.concatenate([pairs(w_a), pairs(w_x)], axis=-1).astype(BF16)


def kernel(x, mem, g_mix, w_in, b_gate, conv_w, conv_b, w_rg_a, b_rg_a, w_rg_x, b_rg_x, lru_lambda, w_lru_out, rel_bias, w_dil_out, g_mem, w_mem_kv, w_mem_out, w_out, g_mlp, w_mlp_in, w_mlp_out, g_final):
    B, S, D = x.shape
    x2 = x.reshape(B * S, D)
    row = lambda v: v.reshape(1, -1)

    qkv0 = 2 * D_RNN
    qm0 = qkv0 + DIL_QKV_WIDTH
    cols = [w_in[:, :qkv0]]
    for g in range(N_DIL_GROUPS):
        part = lambda c: w_in[:, qkv0 + (c * N_DIL_GROUPS + g) * GROUP_WIDTH:
                              qkv0 + (c * N_DIL_GROUPS + g + 1) * GROUP_WIDTH]
        cols += [part(0) * (DIL_HEAD_DIM ** -0.5), part(1), part(2)]
    cols.append(w_in[:, qm0:qm0 + MEM_WIDTH])
    w_a = jnp.concatenate(cols, axis=1).astype(BF16)
    w_g = w_in[:, qm0 + MEM_WIDTH:].astype(BF16)

    mk, mv = _mem_kv(mem, row(g_mem), w_mem_kv.astype(BF16))
    q1, kv1, q4, kv4, q16, kv16, qm, u = _in_proj_lru(
        x2, row(g_mix), w_a, conv_w, row(conv_b), _gate_pair_weights(w_rg_a, w_rg_x),
        row(b_rg_a), row(b_rg_x), row(lru_lambda), S)
    bias = _rel_bias(rel_bias)
    dil_outs = []
    for g, (qg, kvg) in enumerate(((q1, kv1), (q4, kv4), (q16, kv16))):
        dil_outs += _dil_attn(qg, kvg, bias[g * HEADS_PER_GROUP:(g + 1) * HEADS_PER_GROUP], B, S, DIL_GROUPS[g][1])
    x1 = _merge(x2, u, dil_outs, qm, mk, mv, row(g_mix), w_g, row(b_gate),
                w_lru_out.astype(BF16), w_dil_out.astype(BF16), w_mem_out.astype(BF16), w_out.astype(BF16), S)
    y = _mlp(x1, row(g_mlp), w_mlp_in.astype(BF16), w_mlp_out.astype(BF16), row(g_final))
    return y.reshape(B, S, D)
```
